```python
import jax, jax.numpy as jnp
from jax import lax
import numpy as np

D_MODEL = 2048
BATCH = 4
SEQ = 4096
DEPTH = 4

CHUNK = 64
N_A_LAYERS = DEPTH // 2
N_B_LAYERS = DEPTH - N_A_LAYERS
RMS_EPS = 1e-6

GLA_HEADS = 4
GLA_DK = D_MODEL // 2 // GLA_HEADS
GLA_DV = D_MODEL // GLA_HEADS
GLA_GATE_RANK = 16
GLA_GATE_TEMP = 16.0
GLA_IN_COLS = 2 * GLA_HEADS * GLA_DK + 2 * GLA_HEADS * GLA_DV + GLA_GATE_RANK

FOX_HEADS = 16
FOX_DH = D_MODEL // FOX_HEADS
FOX_QBLOCK = 128
FOX_FORGET_BIAS_MEAN = 3.0

PEER_HEADS = 8
PEER_QDIM = 256
PEER_HALF = PEER_QDIM // 2
PEER_NKEYS = 128
PEER_N_EXPERTS = PEER_NKEYS * PEER_NKEYS
PEER_TOPK = 16
PEER_TOKEN_BLOCK = 128

kernel_name = 'hybrid_gla_fox_peer_adaln_trunk'


def _rmsnorm(x, g):
    xf = x.astype(jnp.float32)
    xf = xf * lax.rsqrt(jnp.mean(xf * xf, axis=-1, keepdims=True) + RMS_EPS)
    return xf.astype(x.dtype) * g


def _modulate(h, shift, scale):
    return h * (1 + scale[:, None, :]) + shift[:, None, :]


def _gla(h, w_in, w_gate2, b_gate2, g_norm, w_out):
    B, S, _ = h.shape
    nc = S // CHUNK
    hk = GLA_HEADS * GLA_DK
    hv = GLA_HEADS * GLA_DV
    proj = h @ w_in
    q, k, v, g, gl = jnp.split(proj, [hk, 2 * hk, 2 * hk + hv, 2 * hk + 2 * hv], axis=-1)
    log_a = jax.nn.log_sigmoid((gl @ w_gate2 + b_gate2).astype(jnp.float32)) / GLA_GATE_TEMP

    def to_chunks(t, dh):
        return t.astype(jnp.float32).reshape(B, nc, CHUNK, GLA_HEADS, dh).transpose(1, 0, 3, 2, 4)

    qc = to_chunks(q, GLA_DK) * (GLA_DK ** -0.5)
    kc = to_chunks(k, GLA_DK)
    vc = to_chunks(v, GLA_DV)
    bc = jnp.cumsum(to_chunks(log_a, GLA_DK), axis=3)

    def step(state, xs):
        qt, kt, vt, bt = xs
        inter = jnp.einsum('bhtk,bhkv->bhtv', qt * jnp.exp(bt), state)
        decay = jnp.exp(-jnp.abs(bt[:, :, :, None, :] - bt[:, :, None, :, :]))
        scores = jnp.einsum('bhtk,bhsk,bhtsk->bhts', qt, kt, decay)
        out = inter + jnp.einsum('bhts,bhsv->bhtv', scores, vt)
        b_end = bt[:, :, -1:, :]
        state = (jnp.exp(b_end[:, :, 0, :])[..., None] * state
                 + jnp.einsum('bhsk,bhsv->bhkv', kt * jnp.exp(b_end - bt), vt))
        return state, out

    s0 = jnp.zeros((B, GLA_HEADS, GLA_DK, GLA_DV), jnp.float32)
    _, o = lax.scan(step, s0, (qc, kc, vc, bc))
    o = o.transpose(1, 0, 3, 2, 4).reshape(B, S, GLA_HEADS, GLA_DV)
    o = _rmsnorm(o, g_norm).astype(h.dtype).reshape(B, S, hv)
    return (o * jax.nn.silu(g)) @ w_out


def _shared_kv(x, c_act, kv_norm, kv_ada_w, kv_ada_b, w_kv, b_f):
    B, S, _ = x.shape
    shift, scale = jnp.split(c_act @ kv_ada_w + kv_ada_b, 2, axis=-1)
    hk = _modulate(_rmsnorm(x, kv_norm), shift, scale)
    k, v, fl = jnp.split(hk @ w_kv, [D_MODEL, 2 * D_MODEL], axis=-1)
    log_f = jax.nn.log_sigmoid((fl + b_f).astype(jnp.float32))
    F = jnp.cumsum(log_f, axis=1).transpose(0, 2, 1)
    k = k.reshape(B, S, FOX_HEADS, FOX_DH).transpose(0, 2, 1, 3)
    v = v.reshape(B, S, FOX_HEADS, FOX_DH).transpose(0, 2, 1, 3)
    return k, v, F


def _fox(h, k, v, F, w_q, w_out):
    B, S, _ = h.shape
    q, g = jnp.split(h @ w_q, 2, axis=-1)
    q = q.reshape(B, S, FOX_HEADS, FOX_DH).transpose(0, 2, 1, 3) * (FOX_DH ** -0.5)
    outs = []
    for i in range(S // FOX_QBLOCK):
        lo, hi = i * FOX_QBLOCK, (i + 1) * FOX_QBLOCK
        s = jnp.einsum('bhqd,bhkd->bhqk', q[:, :, lo:hi], k[:, :, :hi]).astype(jnp.float32)
        s = s + F[:, :, lo:hi, None] - F[:, :, None, :hi]
        mask = jnp.arange(lo, hi)[:, None] >= jnp.arange(hi)[None, :]
        p = jax.nn.softmax(jnp.where(mask, s, -jnp.inf), axis=-1).astype(v.dtype)
        outs.append(jnp.einsum('bhqk,bhkd->bhqd', p, v[:, :, :hi]))
    o = jnp.concatenate(outs, axis=2).transpose(0, 2, 1, 3).reshape(B, S, D_MODEL)
    return (o * jax.nn.sigmoid(g)) @ w_out


def _peer(h, w_q, subkeys, u, v):
    B, S, D = h.shape
    T = B * S
    hf = h.reshape(T, D)
    q = (hf @ w_q).reshape(T, PEER_HEADS, 2, PEER_HALF)
    s = jnp.einsum('thpd,hpnd->thpn', q, subkeys)
    vals, idx = lax.top_k(s, PEER_TOPK)
    cand = (vals[:, :, 0, :, None] + vals[:, :, 1, None, :]).reshape(T, PEER_HEADS, PEER_TOPK * PEER_TOPK)
    cand_id = (idx[:, :, 0, :, None] * PEER_NKEYS + idx[:, :, 1, None, :]).reshape(T, PEER_HEADS, PEER_TOPK * PEER_TOPK)
    top_v, top_i = lax.top_k(cand, PEER_TOPK)
    expert = jnp.take_along_axis(cand_id, top_i, axis=-1)
    w = jax.nn.softmax(top_v.astype(jnp.float32), axis=-1).astype(h.dtype)
    nb = T // PEER_TOKEN_BLOCK
    kk = PEER_HEADS * PEER_TOPK
    ids = expert.reshape(nb, PEER_TOKEN_BLOCK, kk)
    ws = w.reshape(nb, PEER_TOKEN_BLOCK, kk)
    xb = hf.reshape(nb, PEER_TOKEN_BLOCK, D)

    def block(args):
        xt, it, wt = args
        act = jax.nn.gelu(jnp.einsum('tkd,td->tk', u[it], xt))
        return jnp.einsum('tk,tkd->td', wt * act, v[it])

    return lax.map(block, (xb, ids, ws)).reshape(B, S, D)


def setup_inputs(seed: int = 0) -> dict:
    key = jax.random.key(seed)
    ks = jax.random.split(key, 24)
    D = D_MODEL

    def nrm(k, shape, std):
        return jax.random.normal(k, shape, jnp.float32) * std

    def gain(k, shape):
        return 1.0 + nrm(k, shape, 0.01)

    return {
        'x': nrm(ks[0], (BATCH, SEQ, D), 1.0),
        'c': nrm(ks[1], (BATCH, D), 1.0),
        'ada_w': nrm(ks[2], (DEPTH, D, 6 * D), 0.5 * D ** -0.5),
        'ada_b': nrm(ks[3], (DEPTH, 6 * D), 0.02),
        'norm_mix': gain(ks[4], (DEPTH, D)),
        'norm_ffn': gain(ks[5], (DEPTH, D)),
        'gla_w_in': nrm(ks[6], (N_A_LAYERS, D, GLA_IN_COLS), D ** -0.5),
        'gla_w_gate2': nrm(ks[7], (N_A_LAYERS, GLA_GATE_RANK, GLA_HEADS * GLA_DK), GLA_GATE_RANK ** -0.5),
        'gla_b_gate2': nrm(ks[8], (N_A_LAYERS, GLA_HEADS * GLA_DK), 0.02),
        'gla_norm': gain(ks[9], (N_A_LAYERS, GLA_DV)),
        'gla_w_out': nrm(ks[10], (N_A_LAYERS, GLA_HEADS * GLA_DV, D), (GLA_HEADS * GLA_DV) ** -0.5),
        'kv_norm': gain(ks[11], (D,)),
        'kv_ada_w': nrm(ks[12], (D, 2 * D), 0.5 * D ** -0.5),
        'kv_ada_b': nrm(ks[13], (2 * D,), 0.02),
        'fox_w_kv': nrm(ks[14], (D, 2 * D + FOX_HEADS), D ** -0.5),
        'fox_b_f': FOX_FORGET_BIAS_MEAN + nrm(ks[15], (FOX_HEADS,), 0.5),
        'fox_w_q': nrm(ks[16], (N_B_LAYERS, D, 2 * D), D ** -0.5),
        'fox_w_out': nrm(ks[17], (N_B_LAYERS, D, D), D ** -0.5),
        'peer_w_q': nrm(ks[18], (DEPTH, D, PEER_HEADS * PEER_QDIM), D ** -0.5),
        'peer_subkeys': nrm(ks[19], (DEPTH, PEER_HEADS, 2, PEER_NKEYS, PEER_HALF), PEER_HALF ** -0.5),
        'peer_u': nrm(ks[20], (DEPTH, PEER_N_EXPERTS, D), D ** -0.5),
        'peer_v': nrm(ks[21], (DEPTH, PEER_N_EXPERTS, D), PEER_HEADS ** -0.5),
        'final_norm': gain(ks[22], (D,)),
    }


def reference(x, c, ada_w, ada_b, norm_mix, norm_ffn, gla_w_in, gla_w_gate2, gla_b_gate2,
              gla_norm, gla_w_out, kv_norm, kv_ada_w, kv_ada_b, fox_w_kv, fox_b_f,
              fox_w_q, fox_w_out, peer_w_q, peer_subkeys, peer_u, peer_v, final_norm):
    c_act = jax.nn.silu(c)
    kv = None
    for l in range(DEPTH):
        mod = c_act @ ada_w[l] + ada_b[l]
        sh_m, sc_m, gt_m, sh_f, sc_f, gt_f = jnp.split(mod, 6, axis=-1)
        h = _modulate(_rmsnorm(x, norm_mix[l]), sh_m, sc_m)
        if l < N_A_LAYERS:
            a = l
            y = _gla(h, gla_w_in[a], gla_w_gate2[a], gla_b_gate2[a], gla_norm[a], gla_w_out[a])
        else:
            b = l - N_A_LAYERS
            y = _fox(h, kv[0], kv[1], kv[2], fox_w_q[b], fox_w_out[b])
        x = x + gt_m[:, None, :] * y
        h = _modulate(_rmsnorm(x, norm_ffn[l]), sh_f, sc_f)
        x = x + gt_f[:, None, :] * _peer(h, peer_w_q[l], peer_subkeys[l], peer_u[l], peer_v[l])
        if l == N_A_LAYERS - 1:
            kv = _shared_kv(x, c_act, kv_norm, kv_ada_w, kv_ada_b, fox_w_kv, fox_b_f)
    return _rmsnorm(x, final_norm)
```

```python
import functools
import math

import jax
import jax.numpy as jnp
from jax import lax
from jax.experimental import pallas as pl
from jax.experimental.pallas import tpu as pltpu

F32 = jnp.float32
BF16 = jnp.bfloat16

RMS_EPS = 1e-6
GLA_CHUNK = 64
GLA_GATE_TEMP = 16.0
PEER_TOPK = 16
LANE = 128
SUBLANE = 8
VMEM_LIMIT = 56 * 1024 * 1024

_NT = (((1,), (1,)), ((), ()))
_TN = (((0,), (0,)), ((), ()))


def _cparams(sem):
    return pltpu.CompilerParams(dimension_semantics=sem, vmem_limit_bytes=VMEM_LIMIT)


def _tile(n, pref):
    if n <= pref:
        return n
    t = pref - pref % LANE
    while n % t:
        t -= LANE
    assert t > 0, (n, pref)
    return t


def _mods_kernel(c_ref, w_ref, b_ref, o_ref):
    c = c_ref[...]
    c_act = c * jax.nn.sigmoid(c)
    o_ref[...] = jnp.dot(c_act, w_ref[...], preferred_element_type=F32,
                         precision=lax.Precision.HIGHEST) + b_ref[...]


def _mods(c_pad, w, b):
    L, D, N = w.shape
    R = c_pad.shape[0]
    tn = _tile(N, 1024)
    return pl.pallas_call(
        _mods_kernel,
        grid=(L, N // tn),
        in_specs=[
            pl.BlockSpec((R, D), lambda l, j: (0, 0)),
            pl.BlockSpec((None, D, tn), lambda l, j: (l, 0, j)),
            pl.BlockSpec((None, 1, tn), lambda l, j: (l, 0, j)),
        ],
        out_specs=pl.BlockSpec((None, R, tn), lambda l, j: (l, 0, j)),
        out_shape=jax.ShapeDtypeStruct((L, R, N), F32),
        compiler_params=_cparams(("arbitrary", "arbitrary")),
        name="adaln_mods",
    )(c_pad, w, b.reshape(L, 1, N))


_NORM_ROWS = 256


def _norm_mod_rows(x, g, sh, sc):
    ms = jnp.mean(x * x, axis=-1, keepdims=True)
    h = x * lax.rsqrt(ms + RMS_EPS) * g
    return h * (1.0 + sc) + sh


def _proj_kernel(x_ref, g_ref, sh_ref, sc_ref, w_ref, o_ref, *rest, emit_h):
    if emit_h:
        h_out_ref, h_ref = rest
    else:
        (h_ref,) = rest
    tm = x_ref.shape[0]

    @pl.when(pl.program_id(1) == 0)
    def _():
        g = g_ref[...]
        sh = sh_ref[...]
        sc = sc_ref[...]

        def rows(r, carry):
            rs = pl.ds(pl.multiple_of(r * _NORM_ROWS, _NORM_ROWS), _NORM_ROWS)
            h = _norm_mod_rows(x_ref[rs, :], g, sh, sc).astype(BF16)
            h_ref[rs, :] = h
            if emit_h:
                h_out_ref[rs, :] = h
            return carry

        lax.fori_loop(0, tm // _NORM_ROWS, rows, 0)

    o_ref[...] = jnp.dot(h_ref[...], w_ref[...],
                         preferred_element_type=F32).astype(o_ref.dtype)


def _proj(x, g, shift, scale, w, out_dtype, seq, emit_h=False, tm=1024, tn=1024):
    T, D = x.shape
    N = w.shape[1]
    B = shift.shape[0]
    tm = _tile(seq, tm)
    tn = _tile(N, tn)
    assert tm % _NORM_ROWS == 0
    per = seq // tm
    out_shape = [jax.ShapeDtypeStruct((T, N), out_dtype)]
    out_specs = [pl.BlockSpec((tm, tn), lambda i, j: (i, j))]
    if emit_h:
        out_shape.append(jax.ShapeDtypeStruct((T, D), BF16))
        out_specs.append(pl.BlockSpec((tm, D), lambda i, j: (i, 0)))
    res = pl.pallas_call(
        functools.partial(_proj_kernel, emit_h=emit_h),
        grid=(T // tm, N // tn),
        in_specs=[
            pl.BlockSpec((tm, D), lambda i, j: (i, 0)),
            pl.BlockSpec((1, D), lambda i, j: (0, 0)),
            pl.BlockSpec((None, 1, D), lambda i, j: (i // per, 0, 0)),
            pl.BlockSpec((None, 1, D), lambda i, j: (i // per, 0, 0)),
            pl.BlockSpec((D, tn), lambda i, j: (0, j)),
        ],
        out_specs=out_specs,
        out_shape=out_shape,
        scratch_shapes=[pltpu.VMEM((tm, D), BF16)],
        compiler_params=_cparams(("arbitrary", "arbitrary")),
        name="norm_mod_proj",
    )(x, g.reshape(1, D), shift.reshape(B, 1, D), scale.reshape(B, 1, D), w)
    return res if emit_h else res[0]


def _out_kernel(y_ref, w_ref, x_ref, gate_ref, o_ref):
    acc = jnp.dot(y_ref[...], w_ref[...], preferred_element_type=F32)
    o_ref[...] = x_ref[...] + gate_ref[...] * acc


def _out_proj(y, w, x, gate, seq, tm=1024, tn=1024):
    T, K = y.shape
    N = w.shape[1]
    B = gate.shape[0]
    tm = _tile(seq, tm)
    tn = _tile(N, tn)
    per = seq // tm
    return pl.pallas_call(
        _out_kernel,
        grid=(T // tm, N // tn),
        in_specs=[
            pl.BlockSpec((tm, K), lambda i, j: (i, 0)),
            pl.BlockSpec((K, tn), lambda i, j: (0, j)),
            pl.BlockSpec((tm, tn), lambda i, j: (i, j)),
            pl.BlockSpec((None, 1, tn), lambda i, j: (i // per, 0, j)),
        ],
        out_specs=pl.BlockSpec((tm, tn), lambda i, j: (i, j)),
        out_shape=jax.ShapeDtypeStruct((T, N), F32),
        compiler_params=_cparams(("arbitrary", "arbitrary")),
        name="out_proj_residual",
    )(y, w, x, gate.reshape(B, 1, N))


def _log_sigmoid(x):
    return jnp.minimum(x, 0.0) - jnp.log1p(jnp.exp(-jnp.abs(x)))


def _gla_kernel(q_ref, k_ref, v_ref, g_ref, gl_ref, wg_ref, bg_ref, gn_ref, y_ref, st_ref,
                *, heads, dk, dv):
    C = GLA_CHUNK
    ts = q_ref.shape[0]
    scale = dk ** -0.5

    @pl.when(pl.program_id(1) == 0)
    def _():
        st_ref[...] = jnp.zeros_like(st_ref)

    row = lax.broadcasted_iota(jnp.int32, (C, C), 0)
    col = lax.broadcasted_iota(jnp.int32, (C, C), 1)
    causal = row >= col
    tri = causal.astype(F32)
    gn = gn_ref[...]

    def chunk(ci, carry):
        r = pl.ds(pl.multiple_of(ci * C, C), C)
        glc = gl_ref[r, :]
        for h in range(heads):
            ks = slice(h * dk, (h + 1) * dk)
            vs = slice(h * dv, (h + 1) * dv)
            logit = jnp.dot(glc, wg_ref[:, ks], preferred_element_type=F32) + bg_ref[:, ks]
            la = _log_sigmoid(logit) * (1.0 / GLA_GATE_TEMP)
            b = jnp.dot(tri, la, preferred_element_type=F32, precision=lax.Precision.HIGHEST)
            b_mid = b[C // 2 - 1:C // 2, :]
            b_end = b[C - 1:C, :]
            q = q_ref[r, ks].astype(F32) * scale
            k = k_ref[r, ks].astype(F32)
            v = v_ref[r, vs]
            d = b - b_mid
            ep = jnp.exp(d)
            en = jnp.exp(-d)
            qe = (q * jnp.exp(b)).astype(BF16)
            a_lo = (q * ep).astype(BF16)
            k_lo = (k * en).astype(BF16)
            a_hi = (q * en).astype(BF16)
            k_hi = (k * ep).astype(BF16)
            kd = (k * jnp.exp(b_end - b)).astype(BF16)
            st = st_ref[h]
            inter = lax.dot_general(qe, st.astype(BF16), _NT, preferred_element_type=F32)
            s_lo = lax.dot_general(a_lo, k_lo, _NT, preferred_element_type=F32)
            s_hi = lax.dot_general(a_hi, k_hi, _NT, preferred_element_type=F32)
            sc = jnp.where(causal, s_lo, s_hi).astype(BF16)
            o = inter + jnp.dot(sc, v, preferred_element_type=F32)
            st_ref[h] = st * jnp.exp(b_end) + lax.dot_general(
                v, kd, _TN, preferred_element_type=F32)
            ms = jnp.mean(o * o, axis=-1, keepdims=True)
            on = o * lax.rsqrt(ms + RMS_EPS) * gn
            g = g_ref[r, vs].astype(F32)
            y_ref[r, vs] = (on * (g * jax.nn.sigmoid(g))).astype(BF16)
        return carry

    lax.fori_loop(0, ts // C, chunk, 0)


def _gla(proj, gl, wg, bg, gn, seq, heads, dk, dv, ts=512):
    T = proj.shape[0]
    hk, hv = heads * dk, heads * dv
    assert (2 * hk) % hv == 0
    vb = 2 * hk // hv
    ts = _tile(seq, ts)
    per = seq // ts
    B = T // seq
    rank = gl.shape[1]
    return pl.pallas_call(
        functools.partial(_gla_kernel, heads=heads, dk=dk, dv=dv),
        grid=(B, per),
        in_specs=[
            pl.BlockSpec((ts, hk), lambda b, s: (b * per + s, 0)),
            pl.BlockSpec((ts, hk), lambda b, s: (b * per + s, 1)),
            pl.BlockSpec((ts, hv), lambda b, s: (b * per + s, vb)),
            pl.BlockSpec((ts, hv), lambda b, s: (b * per + s, vb + 1)),
            pl.BlockSpec((ts, rank), lambda b, s: (b * per + s, 0)),
            pl.BlockSpec((rank, hk), lambda b, s: (0, 0)),
            pl.BlockSpec((1, hk), lambda b, s: (0, 0)),
            pl.BlockSpec((1, dv), lambda b, s: (0, 0)),
        ],
        out_specs=pl.BlockSpec((ts, hv), lambda b, s: (b * per + s, 0)),
        out_shape=jax.ShapeDtypeStruct((T, hv), BF16),
        scratch_shapes=[pltpu.VMEM((heads, dv, dk), F32)],
        compiler_params=_cparams(("arbitrary", "arbitrary")),
        name="gla_chunks",
    )(proj, proj, proj, proj, gl, wg, bg, gn)


def _fcum_kernel(fl_ref, bf_ref, o_ref, carry_ref):
    tb = fl_ref.shape[0]

    @pl.when(pl.program_id(1) == 0)
    def _():
        carry_ref[...] = jnp.zeros_like(carry_ref)

    row = lax.broadcasted_iota(jnp.int32, (tb, tb), 0)
    col = lax.broadcasted_iota(jnp.int32, (tb, tb), 1)
    tri = (row >= col).astype(F32)
    lf = _log_sigmoid(fl_ref[...] + bf_ref[...])
    cs = jnp.dot(tri, lf, preferred_element_type=F32,
                 precision=lax.Precision.HIGHEST) + carry_ref[...]
    o_ref[...] = cs
    carry_ref[...] = cs[tb - 1:tb, :]


def _fcum(fl, bf_pad, seq, tb=512):
    T, W = fl.shape
    tb = _tile(seq, tb)
    per = seq // tb
    return pl.pallas_call(
        _fcum_kernel,
        grid=(T // seq, per),
        in_specs=[
            pl.BlockSpec((tb, W), lambda b, s: (b * per + s, 0)),
            pl.BlockSpec((1, W), lambda b, s: (0, 0)),
        ],
        out_specs=pl.BlockSpec((tb, W), lambda b, s: (b * per + s, 0)),
        out_shape=jax.ShapeDtypeStruct((T, W), F32),
        scratch_shapes=[pltpu.VMEM((1, W), F32)],
        compiler_params=_cparams(("arbitrary", "arbitrary")),
        name="fox_forget_cumsum",
    )(fl, bf_pad)


def _fox_kernel(q_ref, g_ref, k_ref, v_ref, f_ref, y_ref, *, dh):
    tq = q_ref.shape[0]
    qi = pl.program_id(2)
    scale = dh ** -0.5
    q = q_ref[...]
    q0 = pl.multiple_of(qi * tq, tq)
    row = lax.broadcasted_iota(jnp.int32, (tq, tq), 0)
    col = lax.broadcasted_iota(jnp.int32, (tq, tq), 1)
    f_q_row = f_ref[:, pl.ds(q0, tq)]
    f_q = jnp.sum(jnp.where(row == col, f_q_row, 0.0), axis=1, keepdims=True)

    def scores(j0):
        kj = k_ref[pl.ds(j0, tq), :]
        s = lax.dot_general(q, kj, _NT, preferred_element_type=F32) * scale
        return s + f_q - f_ref[:, pl.ds(j0, tq)]

    def update(s, j0, carry):
        m, l, acc = carry
        m_new = jnp.maximum(m, jnp.max(s, axis=1, keepdims=True))
        alpha = jnp.exp(m - m_new)
        p = jnp.exp(s - m_new)
        l = alpha * l + jnp.sum(p, axis=1, keepdims=True)
        vj = v_ref[pl.ds(j0, tq), :]
        acc = alpha * acc + jnp.dot(p.astype(BF16), vj, preferred_element_type=F32)
        return m_new, l, acc

    def body(j, carry):
        j0 = pl.multiple_of(j * tq, tq)
        return update(scores(j0), j0, carry)

    init = (jnp.full((tq, 1), -jnp.inf, F32), jnp.zeros((tq, 1), F32), jnp.zeros((tq, dh), F32))
    carry = lax.fori_loop(0, qi, body, init)
    s = jnp.where(row >= col, scores(q0), -jnp.inf)
    m, l, acc = update(s, q0, carry)
    g = g_ref[...].astype(F32)
    y_ref[...] = (acc / l * jax.nn.sigmoid(g)).astype(BF16)


def _fox(qg, kv, f_rows, seq, heads, dh, tq=512):
    T = qg.shape[0]
    B = T // seq
    tq = _tile(seq, tq)
    per = seq // tq
    return pl.pallas_call(
        functools.partial(_fox_kernel, dh=dh),
        grid=(B, heads, per),
        in_specs=[
            pl.BlockSpec((tq, dh), lambda b, h, i: (b * per + i, h)),
            pl.BlockSpec((tq, dh), lambda b, h, i: (b * per + i, heads + h)),
            pl.BlockSpec((seq, dh), lambda b, h, i: (b, h)),
            pl.BlockSpec((seq, dh), lambda b, h, i: (b, heads + h)),
            pl.BlockSpec((None, None, 1, seq), lambda b, h, i: (b, h, 0, 0)),
        ],
        out_specs=pl.BlockSpec((tq, dh), lambda b, h, i: (b * per + i, h)),
        out_shape=jax.ShapeDtypeStruct((T, heads * dh), BF16),
        compiler_params=_cparams(("arbitrary", "arbitrary", "arbitrary")),
        name="fox_attention",
    )(qg, qg, kv, kv, f_rows)


_VROWS = 24


def _extract_top(cur, n, scr=None):
    vals = []
    for r in range(n):
        m = jnp.max(cur, axis=0, keepdims=True)
        if scr is not None:
            scr[r:r + 1, :] = m
        vals.append(m)
        cur = jnp.where(cur == m, -jnp.inf, cur)
    return vals


def _route_kernel(q_ref, sk_ref, s1_ref, th_ref, a_ref, v0_ref, v1_ref, *, half):
    K = PEER_TOPK
    tm = q_ref.shape[0]
    s0 = lax.dot_general(sk_ref[0], q_ref[:, :half], _NT, preferred_element_type=F32)
    s1 = lax.dot_general(sk_ref[1], q_ref[:, half:], _NT, preferred_element_type=F32)
    for s, scr in ((s0, v0_ref), (s1, v1_ref)):
        scr[...] = jnp.full(scr.shape, -jnp.inf, F32)
        _extract_top(s, K + 1, scr)
    v0 = v0_ref[...]
    v1 = v1_ref[...]
    blocks = [v0[0:1] + v1]
    sub = lax.broadcasted_iota(jnp.int32, (SUBLANE, tm), 0)
    for a in range(1, SUBLANE):
        nb = (K + 1) // (a + 1)
        blocks.append(jnp.where(sub < nb, v0[a:a + 1] + v1[0:SUBLANE], -jnp.inf))
    blocks.append(v0[SUBLANE:] + v1[0:1])
    cand = jnp.concatenate(blocks, axis=0)
    top = _extract_top(cand, K + 1)
    tau = 0.5 * (top[K - 1] + top[K])
    cmax = v0[0:1] + v1[0:1]
    z = jnp.sum(jnp.where(cand > tau, jnp.exp(cand - cmax), 0.0), axis=0, keepdims=True)
    s1_ref[...] = s1 - v1[0:1]
    th_ref[...] = (tau - v1[0:1]) - s0
    a_ref[...] = jnp.exp(s0 - v0[0:1]) / z


def _route(q, sk, tm=512):
    T = q.shape[0]
    heads, _, nkeys, half = sk.shape
    assert K1_ROWS_OK(nkeys)
    tm = _tile(T, tm)
    shp = jax.ShapeDtypeStruct((heads, nkeys, T), F32)
    ospec = pl.BlockSpec((None, nkeys, tm), lambda i, h: (h, 0, i))
    return pl.pallas_call(
        functools.partial(_route_kernel, half=half),
        grid=(T // tm, heads),
        in_specs=[
            pl.BlockSpec((tm, 2 * half), lambda i, h: (i, h)),
            pl.BlockSpec((None, 2, nkeys, half), lambda i, h: (h, 0, 0, 0)),
        ],
        out_specs=[ospec, ospec, ospec],
        out_shape=[shp, shp, shp],
        scratch_shapes=[pltpu.VMEM((_VROWS, tm), F32), pltpu.VMEM((_VROWS, tm), F32)],
        compiler_params=_cparams(("arbitrary", "arbitrary")),
        name="peer_route",
    )(q, sk)


def K1_ROWS_OK(nkeys):
    return nkeys >= _VROWS and nkeys % SUBLANE == 0


def _dense_kernel(h_ref, u_ref, vt_ref, s1_ref, th_ref, a_ref, x_ref, gate_ref, o_ref,
                  acc_ref, e1_ref, pre_ref, ga_ref, *, heads, nkeys):
    ei = pl.program_id(1)
    te, tm = pre_ref.shape
    nb = te // nkeys

    @pl.when(ei == 0)
    def _():
        acc_ref[...] = jnp.zeros_like(acc_ref)
        e1_ref[...] = jnp.exp(s1_ref[...])

    pre_ref[...] = lax.dot_general(u_ref[...], h_ref[...], _NT, preferred_element_type=F32)
    for bi in range(nb):
        rs = slice(bi * nkeys, (bi + 1) * nkeys)
        for lc in range(tm // LANE):
            ls = slice(lc * LANE, (lc + 1) * LANE)
            gsum = jnp.zeros((nkeys, LANE), F32)
            for hd in range(heads):
                th = th_ref[hd, bi:bi + 1, ls]
                a = a_ref[hd, bi:bi + 1, ls]
                gsum = gsum + jnp.where(s1_ref[hd, :, ls] >= th, e1_ref[hd, :, ls], 0.0) * a
            act = jax.nn.gelu(pre_ref[rs, ls], approximate=True)
            ga_ref[rs, ls] = (gsum * act).astype(BF16)
    acc_ref[...] += jnp.dot(vt_ref[...], ga_ref[...], preferred_element_type=F32)

    @pl.when(ei == pl.num_programs(1) - 1)
    def _():
        o_ref[...] = x_ref[...] + gate_ref[...] * acc_ref[...].T


def _peer_dense(h, u, vt, s1, th, a, x, gate, seq, tm=512, te=1024):
    T, D = h.shape
    E = u.shape[0]
    heads, nkeys, _ = s1.shape
    B = gate.shape[0]
    tm = _tile(seq, tm)
    te = _tile(E, te)
    nb = te // nkeys
    assert te % nkeys == 0 and tm % LANE == 0 and nb % SUBLANE == 0
    per = seq // tm
    rspec = pl.BlockSpec((heads, nkeys, tm), lambda i, e: (0, 0, i))
    bspec = pl.BlockSpec((heads, nb, tm), lambda i, e: (0, e, i))
    return pl.pallas_call(
        functools.partial(_dense_kernel, heads=heads, nkeys=nkeys),
        grid=(T // tm, E // te),
        in_specs=[
            pl.BlockSpec((tm, D), lambda i, e: (i, 0)),
            pl.BlockSpec((te, D), lambda i, e: (e, 0)),
            pl.BlockSpec((D, te), lambda i, e: (0, e)),
            rspec, bspec, bspec,
            pl.BlockSpec((tm, D), lambda i, e: (i, 0)),
            pl.BlockSpec((None, 1, D), lambda i, e: (i // per, 0, 0)),
        ],
        out_specs=pl.BlockSpec((tm, D), lambda i, e: (i, 0)),
        out_shape=jax.ShapeDtypeStruct((T, D), F32),
        scratch_shapes=[
            pltpu.VMEM((D, tm), F32),
            pltpu.VMEM((heads, nkeys, tm), F32),
            pltpu.VMEM((te, tm), F32),
            pltpu.VMEM((te, tm), BF16),
        ],
        compiler_params=_cparams(("arbitrary", "arbitrary")),
        name="peer_dense",
    )(h, u, vt, s1, th, a, x, gate.reshape(B, 1, D))


def _final_kernel(x_ref, g_ref, o_ref):
    x = x_ref[...]
    ms = jnp.mean(x * x, axis=-1, keepdims=True)
    o_ref[...] = x * lax.rsqrt(ms + RMS_EPS) * g_ref[...]


def _final_norm(x, g, tm=256):
    T, D = x.shape
    tm = _tile(T, tm)
    return pl.pallas_call(
        _final_kernel,
        grid=(T // tm,),
        in_specs=[pl.BlockSpec((tm, D), lambda i: (i, 0)), pl.BlockSpec((1, D), lambda i: (0, 0))],
        out_specs=pl.BlockSpec((tm, D), lambda i: (i, 0)),
        out_shape=jax.ShapeDtypeStruct((T, D), F32),
        compiler_params=_cparams(("arbitrary",)),
        name="final_rmsnorm",
    )(x, g.reshape(1, D))


def _pad_cols(w, n):
    return jnp.pad(w, ((0, 0), (0, n - w.shape[1])))


def kernel(x, c, ada_w, ada_b, norm_mix, norm_ffn, gla_w_in, gla_w_gate2, gla_b_gate2, gla_norm, gla_w_out, kv_norm, kv_ada_w, kv_ada_b, fox_w_kv, fox_b_f, fox_w_q, fox_w_out, peer_w_q, peer_subkeys, peer_u, peer_v, final_norm):
    B, S, D = x.shape
    T = B * S
    depth = ada_w.shape[0]
    n_a = gla_w_in.shape[0]
    gla_dv = gla_norm.shape[1]
    gla_hk = gla_w_gate2.shape[2]
    gla_rank = gla_w_gate2.shape[1]
    gla_hv = gla_w_out.shape[1]
    gla_heads = gla_hv // gla_dv
    gla_dk = gla_hk // gla_heads
    assert gla_w_in.shape[2] == 2 * gla_hk + 2 * gla_hv + gla_rank
    fox_heads = fox_b_f.shape[0]
    fox_dh = D // fox_heads
    assert gla_rank <= LANE and fox_heads <= LANE

    xs = x.reshape(T, D)
    c_pad = jnp.pad(c, ((0, SUBLANE - B % SUBLANE if B % SUBLANE else 0), (0, 0)))
    mods = _mods(c_pad, ada_w, ada_b)[:, :B]
    kv_mods = _mods(c_pad, kv_ada_w[None], kv_ada_b[None])[0, :B]

    kv = None
    f_rows = None
    for l in range(depth):
        sh_m, sc_m, gt_m, sh_f, sc_f, gt_f = [mods[l, :, i * D:(i + 1) * D] for i in range(6)]
        if l < n_a:
            w_in = gla_w_in[l]
            n_main = 2 * gla_hk + 2 * gla_hv
            proj = _proj(xs, norm_mix[l], sh_m, sc_m, w_in[:, :n_main].astype(BF16), BF16, S)
            gl = _proj(xs, norm_mix[l], sh_m, sc_m,
                       _pad_cols(w_in[:, n_main:], LANE).astype(BF16), BF16, S)
            wg = jnp.pad(gla_w_gate2[l], ((0, LANE - gla_rank), (0, 0))).astype(BF16)
            y = _gla(proj, gl, wg, gla_b_gate2[l].reshape(1, gla_hk), gla_norm[l].reshape(1, gla_dv),
                     S, gla_heads, gla_dk, gla_dv)
            xs = _out_proj(y, gla_w_out[l].astype(BF16), xs, gt_m, S)
        else:
            b = l - n_a
            qg = _proj(xs, norm_mix[l], sh_m, sc_m, fox_w_q[b].astype(BF16), BF16, S)
            y = _fox(qg, kv, f_rows, S, fox_heads, fox_dh)
            xs = _out_proj(y, fox_w_out[b].astype(BF16), xs, gt_m, S)

        q, h = _proj(xs, norm_ffn[l], sh_f, sc_f, peer_w_q[l].astype(BF16), BF16, S, emit_h=True)
        s1, th, a = _route(q, peer_subkeys[l].astype(BF16))
        xs = _peer_dense(h, peer_u[l].astype(BF16), peer_v[l].T.astype(BF16), s1, th, a, xs, gt_f, S)

        if l == n_a - 1:
            sh_k, sc_k = kv_mods[:, :D], kv_mods[:, D:]
            kv = _proj(xs, kv_norm, sh_k, sc_k, fox_w_kv[:, :2 * D].astype(BF16), BF16, S)
            fl = _proj(xs, kv_norm, sh_k, sc_k, _pad_cols(fox_w_kv[:, 2 * D:], LANE).astype(BF16), F32, S)
            bf_pad = jnp.pad(fox_b_f, (0, LANE - fox_heads)).reshape(1, LANE)
            fc = _fcum(fl, bf_pad, S)
            f_rows = fc.reshape(B, S, LANE)[:, :, :fox_heads].transpose(0, 2, 1)[:, :, None, :]

    return _final_norm(xs, final_norm).reshape(B, S, D)
```
